```python
import jax, jax.numpy as jnp
from jax import lax
import numpy as np

D_MODEL = 1024
BATCH = 8
SEQ = 2048
DEPTH = 4
DEC_BATCH = 128
DEC_SEQ = 8
PAST_LEN = 16384
PAGE_SIZE = 128

N_META = 16
RET_HEADS = 4
RET_DK = 256
RET_DV = 512
RET_QK = RET_HEADS * RET_DK
RET_V = RET_HEADS * RET_DV
CHUNK = 128
ROPE_BASE = 10000.0
LRU_WIDTH = 1536
LRU_BLOCKS = 16
LRU_BLOCK = LRU_WIDTH // LRU_BLOCKS
CONV_W = 4
LRU_C = 8.0
EPS = 1e-6
SPLIT_IDX = (RET_QK,
             2 * RET_QK,
             2 * RET_QK + RET_V,
             2 * RET_QK + 2 * RET_V,
             2 * RET_QK + 2 * RET_V + LRU_WIDTH,
             2 * RET_QK + 2 * RET_V + 2 * LRU_WIDTH,
             2 * RET_QK + 2 * RET_V + 2 * LRU_WIDTH + D_MODEL)
IN_COLS = 2 * RET_QK + 2 * RET_V + 2 * LRU_WIDTH + 2 * D_MODEL

kernel_name = "hybrid_retention_rglru_meta_step"


def rmsnorm(x, g):
    xf = x.astype(jnp.float32)
    y = xf * lax.rsqrt(jnp.mean(xf * xf, axis=-1, keepdims=True) + EPS)
    return (y * g.astype(jnp.float32)).astype(x.dtype)


def rope(x, pos):
    half = x.shape[-1] // 2
    inv = ROPE_BASE ** (-jnp.arange(half, dtype=jnp.float32) / half)
    ang = pos.astype(jnp.float32)[:, None] * inv[None, :]
    cos = jnp.cos(ang)[None, :, None, :]
    sin = jnp.sin(ang)[None, :, None, :]
    xf = x.astype(jnp.float32)
    x1, x2 = xf[..., :half], xf[..., half:]
    return jnp.concatenate([x1 * cos - x2 * sin, x2 * cos + x1 * sin], axis=-1)


def retention_log_gamma():
    return jnp.log1p(-(2.0 ** (-5.0 - jnp.arange(RET_HEADS, dtype=jnp.float32))))


def retention_chunk(q, k, v, S, log_g):
    L = q.shape[1]
    idx = jnp.arange(L, dtype=jnp.float32)
    diff = idx[:, None] - idx[None, :]
    decay = jnp.where((diff >= 0)[None], jnp.exp(log_g[:, None, None] * jnp.maximum(diff, 0.0)[None]), 0.0)
    scores = jnp.einsum('blhd,bmhd->bhlm', q, k) * decay[None]
    inner = jnp.einsum('bhlm,bmhe->blhe', scores, v)
    cross_decay = jnp.exp(log_g[None, :] * (idx[:, None] + 1.0))
    cross = jnp.einsum('blhd,bhde->blhe', q, S) * cross_decay[None, :, :, None]
    state_decay = jnp.exp(log_g[None, :] * (L - 1.0 - idx)[:, None])
    S_new = jnp.exp(log_g * L)[None, :, None, None] * S + jnp.einsum(
        'blhd,blhe->bhde', k * state_decay[None, :, :, None], v)
    return inner + cross, S_new


def retention_prompt(q, k, v, log_g):
    B = q.shape[0]
    S0 = jnp.zeros((B, RET_HEADS, RET_DK, RET_DV), jnp.float32)
    o_meta, S = retention_chunk(q[:, :N_META], k[:, :N_META], v[:, :N_META], S0, log_g)

    def to_chunks(t):
        t = t[:, N_META:]
        n = t.shape[1] // CHUNK
        return t.reshape((B, n, CHUNK) + t.shape[2:]).swapaxes(0, 1)

    def body(S_c, qkv):
        qc, kc, vc = qkv
        o, S_n = retention_chunk(qc, kc, vc, S_c, log_g)
        return S_n, o

    S, o = lax.scan(body, S, (to_chunks(q), to_chunks(k), to_chunks(v)))
    o = o.swapaxes(0, 1).reshape(B, -1, RET_HEADS, RET_DV)
    return jnp.concatenate([o_meta, o], axis=1), S


def group_norm(o, g):
    mu = jnp.mean(o, axis=-1, keepdims=True)
    var = jnp.mean(jnp.square(o - mu), axis=-1, keepdims=True)
    y = (o - mu) * lax.rsqrt(var + EPS)
    B, L = o.shape[0], o.shape[1]
    return y.reshape(B, L, RET_V) * g.astype(jnp.float32)


def causal_conv(xb, buf, w, b):
    L = xb.shape[1]
    xp = jnp.concatenate([buf.astype(xb.dtype), xb], axis=1)
    y = b + sum(xp[:, j:j + L] * w[j] for j in range(CONV_W))
    return y, xp[:, -(CONV_W - 1):]


def rglru(xc, h0, w_r, b_r, w_i, b_i, lam):
    B, L, W = xc.shape
    xf = xc.astype(jnp.float32)
    xblk = xf.reshape(B, L, LRU_BLOCKS, LRU_BLOCK)
    r = jax.nn.sigmoid(jnp.einsum('blnc,ncd->blnd', xblk, w_r.astype(jnp.float32)).reshape(B, L, W) + b_r.astype(jnp.float32))
    i = jax.nn.sigmoid(jnp.einsum('blnc,ncd->blnd', xblk, w_i.astype(jnp.float32)).reshape(B, L, W) + b_i.astype(jnp.float32))
    log_a = -LRU_C * r * jax.nn.softplus(-lam.astype(jnp.float32))
    a = jnp.exp(log_a)
    bterm = jnp.sqrt(-jnp.expm1(2.0 * log_a)) * (i * xf)
    bterm = bterm.at[:, 0].add(a[:, 0] * h0.astype(jnp.float32))

    def comb(left, right):
        a1, b1 = left
        a2, b2 = right
        return a1 * a2, a2 * b1 + b2

    _, h = lax.associative_scan(comb, (a, bterm), axis=1)
    return h, h[:, -1]


def mixer_layer(x, pos, S0, conv_buf, h0, is_prompt, ln_g, w_in, cw, cb, w_r, b_r, w_i, b_i, lam, gn_g, w_a, w_b, w_o):
    B, L, _ = x.shape
    xn = rmsnorm(x, ln_g)
    proj = xn @ w_in
    q, k, v, z_a, x_b, z_b, g_a, g_b = jnp.split(proj, SPLIT_IDX, axis=-1)
    q = rope(q.reshape(B, L, RET_HEADS, RET_DK), pos)
    k = rope(k.reshape(B, L, RET_HEADS, RET_DK), pos) * (RET_DK ** -0.5)
    v = v.reshape(B, L, RET_HEADS, RET_DV).astype(jnp.float32)
    log_g = retention_log_gamma()
    if is_prompt:
        ret, S_new = retention_prompt(q, k, v, log_g)
    else:
        ret, S_new = retention_chunk(q, k, v, S0.astype(jnp.float32), log_g)
    o_a = group_norm(ret, gn_g).astype(x.dtype) * jax.nn.silu(z_a)
    xc, conv_new = causal_conv(x_b, conv_buf, cw, cb)
    h, h_last = rglru(xc, h0, w_r, b_r, w_i, b_i, lam)
    o_b = h.astype(x.dtype) * jax.nn.silu(z_b)
    merged = jax.nn.sigmoid(g_a) * (o_a @ w_a) + jax.nn.sigmoid(g_b) * (o_b @ w_b)
    return x + merged @ w_o, S_new, conv_new, h_last


def setup_inputs(seed: int = 0) -> dict:
    key = jax.random.key(seed)
    ks = jax.random.split(key, 24)
    f32 = jnp.float32
    a0 = jax.random.uniform(ks[12], (DEPTH, LRU_WIDTH), f32, 0.9, 0.999)
    return {
        "x_prompt": jax.random.normal(ks[0], (BATCH, SEQ, D_MODEL), f32),
        "x_sample": jax.random.normal(ks[1], (DEC_BATCH, DEC_SEQ, D_MODEL), f32),
        "state_ret": 0.3 * jax.random.normal(ks[2], (DEPTH, DEC_BATCH, RET_HEADS, RET_DK, RET_DV), f32),
        "state_conv": jax.random.normal(ks[3], (DEPTH, DEC_BATCH, CONV_W - 1, LRU_WIDTH), f32),
        "state_rglru": 0.5 * jax.random.normal(ks[4], (DEPTH, DEC_BATCH, LRU_WIDTH), f32),
        "meta_tokens": jax.random.normal(ks[5], (N_META, D_MODEL), f32),
        "ln_gain": 1.0 + 0.02 * jax.random.normal(ks[6], (DEPTH, D_MODEL), f32),
        "w_in": jax.random.normal(ks[7], (DEPTH, D_MODEL, IN_COLS), f32) * D_MODEL ** -0.5,
        "conv_w": jax.random.normal(ks[8], (DEPTH, CONV_W, LRU_WIDTH), f32) * CONV_W ** -0.5,
        "conv_b": 0.01 * jax.random.normal(ks[9], (DEPTH, LRU_WIDTH), f32),
        "w_r": jax.random.normal(ks[10], (DEPTH, LRU_BLOCKS, LRU_BLOCK, LRU_BLOCK), f32) * LRU_BLOCK ** -0.5,
        "b_r": 0.01 * jax.random.normal(ks[11], (DEPTH, LRU_WIDTH), f32),
        "w_i": jax.random.normal(ks[13], (DEPTH, LRU_BLOCKS, LRU_BLOCK, LRU_BLOCK), f32) * LRU_BLOCK ** -0.5,
        "b_i": 0.01 * jax.random.normal(ks[14], (DEPTH, LRU_WIDTH), f32),
        "lru_lambda": jnp.log(a0) - jnp.log1p(-a0),
        "gn_gain": 1.0 + 0.02 * jax.random.normal(ks[15], (DEPTH, RET_V), f32),
        "w_a": jax.random.normal(ks[16], (DEPTH, RET_V, D_MODEL), f32) * RET_V ** -0.5,
        "w_b": jax.random.normal(ks[17], (DEPTH, LRU_WIDTH, D_MODEL), f32) * LRU_WIDTH ** -0.5,
        "w_o": jax.random.normal(ks[18], (DEPTH, D_MODEL, D_MODEL), f32) * D_MODEL ** -0.5,
        "final_gain": 1.0 + 0.02 * jax.random.normal(ks[19], (D_MODEL,), f32),
    }


def reference(x_prompt, x_sample, state_ret, state_conv, state_rglru, meta_tokens, ln_gain, w_in, conv_w, conv_b,
              w_r, b_r, w_i, b_i, lru_lambda, gn_gain, w_a, w_b, w_o, final_gain):
    B = x_prompt.shape[0]
    meta = jnp.broadcast_to(meta_tokens.astype(x_prompt.dtype)[None], (B, N_META, D_MODEL))
    xp = jnp.concatenate([meta, x_prompt], axis=1)
    pos_p = jnp.arange(xp.shape[1], dtype=jnp.int32)
    xs = x_sample
    pos_s = PAST_LEN + jnp.arange(xs.shape[1], dtype=jnp.int32)
    ret_p, conv_p, h_p, ret_s, conv_s, h_s = [], [], [], [], [], []
    for l in range(DEPTH):
        params = (ln_gain[l], w_in[l], conv_w[l], conv_b[l], w_r[l], b_r[l], w_i[l], b_i[l], lru_lambda[l],
                  gn_gain[l], w_a[l], w_b[l], w_o[l])
        xp, S_pl, c_pl, h_pl = mixer_layer(
            xp, pos_p, None, jnp.zeros((B, CONV_W - 1, LRU_WIDTH), xp.dtype),
            jnp.zeros((B, LRU_WIDTH), jnp.float32), True, *params)
        xs, S_sl, c_sl, h_sl = mixer_layer(
            xs, pos_s, state_ret[l], state_conv[l], state_rglru[l], False, *params)
        ret_p.append(S_pl); conv_p.append(c_pl); h_p.append(h_pl)
        ret_s.append(S_sl); conv_s.append(c_sl); h_s.append(h_sl)
    y_prompt = rmsnorm(xp, final_gain)[:, N_META:]
    y_sample = rmsnorm(xs, final_gain)
    return (y_prompt, y_sample,
            jnp.stack(ret_p), jnp.stack(conv_p), jnp.stack(h_p),
            jnp.stack(ret_s), jnp.stack(conv_s), jnp.stack(h_s))
```

```python
import functools
import math

import numpy as np
import jax
import jax.numpy as jnp
from jax import lax
from jax.experimental import pallas as pl
from jax.experimental.pallas import tpu as pltpu

D_MODEL = 1024
DEPTH = 4
N_META = 16
RET_HEADS = 4
RET_DK = 256
RET_DV = 512
RET_QK = RET_HEADS * RET_DK
RET_V = RET_HEADS * RET_DV
CHUNK = 128
ROPE_BASE = 10000.0
LRU_WIDTH = 1536
LRU_BLOCKS = 16
LRU_BLOCK = LRU_WIDTH // LRU_BLOCKS
CONV_W = 4
LRU_C = 8.0
EPS = 1e-6
IN_COLS = 2 * RET_QK + 2 * RET_V + 2 * LRU_WIDTH + 2 * D_MODEL
PAST_LEN = 16384

COL_Q = 0
COL_K = 1
COL_V = 1
COL_ZA = 2
COL_XB = 4
COL_ZB = 5
COL_GA = 9
COL_GB = 10

LANES = 128
SUBLANES = 8
GATE_SUPER = 4 * LRU_BLOCK
N_SUPER = LRU_WIDTH // GATE_SUPER
PROJ_TN = 1024
VMEM_LIMIT = 56 * 1024 * 1024

F32 = jnp.float32
BF16 = jnp.bfloat16


def _params(n_axes):
    return pltpu.CompilerParams(
        dimension_semantics=("arbitrary",) * n_axes, vmem_limit_bytes=VMEM_LIMIT)


def _sigmoid(x):
    return 1.0 / (1.0 + jnp.exp(-x))


def _inproj_kernel(x_ref, g_ref, w_ref, cos_ref, sin_ref, o_ref, xn_ref):
    j = pl.program_id(1)

    @pl.when(j == 0)
    def _():
        x = x_ref[...]
        ms = jnp.mean(x * x, axis=-1, keepdims=True)
        xn_ref[...] = (x * lax.rsqrt(ms + EPS) * g_ref[...]).astype(BF16)

    acc = jnp.dot(xn_ref[...], w_ref[...], preferred_element_type=F32)

    @pl.when(j <= COL_K)
    def _():
        scale = jnp.where(j == COL_K, RET_DK ** -0.5, 1.0).astype(F32)
        cos = cos_ref[...] * scale
        sin = sin_ref[...] * scale
        half = RET_DK // 2
        for h in range(RET_HEADS):
            lo = h * RET_DK
            x1 = acc[:, lo:lo + half]
            x2 = acc[:, lo + half:lo + RET_DK]
            o_ref[:, lo:lo + half] = (x1 * cos - x2 * sin).astype(BF16)
            o_ref[:, lo + half:lo + RET_DK] = (x2 * cos + x1 * sin).astype(BF16)

    @pl.when(j > COL_K)
    def _():
        o_ref[...] = acc.astype(BF16)


def _inproj(x, ln_g, w_in_b, layer, cos, sin, tm):
    t = x.shape[0]
    n_cos = cos.shape[0] // tm
    return pl.pallas_call(
        _inproj_kernel,
        grid=(t // tm, IN_COLS // PROJ_TN),
        in_specs=[
            pl.BlockSpec((tm, D_MODEL), lambda i, j: (i, 0)),
            pl.BlockSpec((1, D_MODEL), lambda i, j: (0, 0)),
            pl.BlockSpec((None, D_MODEL, PROJ_TN), lambda i, j: (layer, 0, j)),
            pl.BlockSpec((tm, LANES), lambda i, j: (i % n_cos, 0)),
            pl.BlockSpec((tm, LANES), lambda i, j: (i % n_cos, 0)),
        ],
        out_specs=pl.BlockSpec((tm, PROJ_TN), lambda i, j: (i, j)),
        out_shape=jax.ShapeDtypeStruct((t, IN_COLS), BF16),
        scratch_shapes=[pltpu.VMEM((tm, D_MODEL), BF16)],
        compiler_params=_params(2),
        name="inproj",
    )(x, ln_g, w_in_b, cos, sin)


def _ret_consts(ns, seq_len):
    rows = ns * seq_len
    log_g = np.log1p(-(2.0 ** (-5.0 - np.arange(RET_HEADS, dtype=np.float64))))
    r = np.arange(rows)
    seq = r // seq_len
    pos = (r % seq_len).astype(np.float64)
    diff = pos[:, None] - pos[None, :]
    same = seq[:, None] == seq[None, :]
    dec = np.where((same & (diff >= 0))[None],
                   np.exp(log_g[:, None, None] * np.maximum(diff, 0.0)[None]), 0.0)
    cross = np.exp(log_g[:, None] * (pos[None, :] + 1.0))
    state = np.exp(log_g[:, None] * (seq_len - 1.0 - pos)[None, :])
    in_seq = (seq[None, :] == np.arange(ns)[:, None]).astype(np.float64)
    cd = in_seq[:, None, :, None] * cross[None, :, :, None] * np.ones((1, 1, 1, RET_DV))
    sd = in_seq[:, None, :, None] * state[None, :, :, None] * np.ones((1, 1, 1, RET_DK))
    chunk_decay = [float(v) for v in np.exp(log_g * seq_len)]
    return (jnp.asarray(dec, F32), jnp.asarray(cd, F32), jnp.asarray(sd, F32), chunk_decay)


def _ret_kernel(*refs, ns, n_chunks, chunk_decay, aliased):
    q_ref, k_ref, v_ref, z_ref, s0_ref, dec_ref, cd_ref, sd_ref, gn_ref = refs[:9]
    refs = refs[9 + (1 if aliased else 0):]
    o_ref, snew_ref = refs[:2]
    carried = n_chunks > 1
    if carried:
        s_scr = refs[2]
        c = pl.program_id(1)

        @pl.when(c == 0)
        def _():
            s_scr[...] = s0_ref[...]

        s_src = s_scr
        s_dst = s_scr
    else:
        s_src = s0_ref
        s_dst = snew_ref

    for h in range(RET_HEADS):
        q = q_ref[:, h * RET_DK:(h + 1) * RET_DK]
        k = k_ref[:, h * RET_DK:(h + 1) * RET_DK]
        v = v_ref[:, h * RET_DV:(h + 1) * RET_DV]
        scores = lax.dot_general(q, k, (((1,), (1,)), ((), ())), preferred_element_type=F32)
        p = (scores * dec_ref[h]).astype(BF16)
        ret = jnp.dot(p, v, preferred_element_type=F32)
        kf = k.astype(F32)
        for s in range(ns):
            state = s_src[s, h]
            ret = ret + jnp.dot(q, state.astype(BF16), preferred_element_type=F32) * cd_ref[s, h]
            kd = (kf * sd_ref[s, h]).astype(BF16)
            s_dst[s, h] = chunk_decay[h] * state + lax.dot_general(
                kd, v, (((0,), (0,)), ((), ())), preferred_element_type=F32)
        mu = jnp.mean(ret, axis=-1, keepdims=True)
        d = ret - mu
        var = jnp.mean(d * d, axis=-1, keepdims=True)
        y = d * lax.rsqrt(var + EPS) * gn_ref[:, h * RET_DV:(h + 1) * RET_DV]
        z = z_ref[:, h * RET_DV:(h + 1) * RET_DV].astype(F32)
        o_ref[:, h * RET_DV:(h + 1) * RET_DV] = (y * (z * _sigmoid(z))).astype(BF16)

    if carried:
        @pl.when(c == n_chunks - 1)
        def _():
            snew_ref[...] = s_scr[...]


def _retention(proj, gn_g, s0, s0_index, *, n_seq, seq_len, ns, chunk, snew_shape, snew_index,
               snew_prev=None):
    n_chunks = seq_len // chunk
    rows = ns * chunk
    assert n_chunks == 1 or ns == 1
    dec, cd, sd, chunk_decay = _ret_consts(ns, chunk)
    row_map = lambda col: (lambda i, c: (i * n_chunks + c, col))
    const4 = lambda i, c: (0, 0, 0, 0)
    in_specs = [
        pl.BlockSpec((rows, RET_QK), row_map(COL_Q)),
        pl.BlockSpec((rows, RET_QK), row_map(COL_K)),
        pl.BlockSpec((rows, RET_V), row_map(COL_V)),
        pl.BlockSpec((rows, RET_V), row_map(COL_ZA)),
        s0_index,
        pl.BlockSpec(dec.shape, lambda i, c: (0, 0, 0)),
        pl.BlockSpec(cd.shape, const4),
        pl.BlockSpec(sd.shape, const4),
        pl.BlockSpec((1, RET_V), lambda i, c: (0, 0)),
    ]
    args = [proj, proj, proj, proj, s0, dec, cd, sd, gn_g]
    aliases = {}
    if snew_prev is not None:
        in_specs.append(pl.BlockSpec(memory_space=pl.ANY))
        args.append(snew_prev)
        aliases = {len(args) - 1: 1}
    scratch = [pltpu.VMEM((ns, RET_HEADS, RET_DK, RET_DV), F32)] if n_chunks > 1 else []
    return pl.pallas_call(
        functools.partial(_ret_kernel, ns=ns, n_chunks=n_chunks, chunk_decay=chunk_decay,
                          aliased=snew_prev is not None),
        grid=(n_seq // ns, n_chunks),
        in_specs=in_specs,
        out_specs=[pl.BlockSpec((rows, RET_V), row_map(0)), snew_index],
        out_shape=[jax.ShapeDtypeStruct((n_seq * seq_len, RET_V), BF16),
                   jax.ShapeDtypeStruct(snew_shape, F32)],
        scratch_shapes=scratch,
        input_output_aliases=aliases,
        compiler_params=_params(2),
        name="retention",
    )(*args)


def _lru_kernel(xb_ref, zb_ref, p0_ref, h0_ref, cw_ref, cb_ref, wg_ref, br_ref, bi_ref, lam_ref,
                o_ref, tail_ref, hlast_ref, *scratch, ns, n_chunks):
    rows = xb_ref.shape[0]
    groups = rows // SUBLANES
    gps = groups // ns
    carried = n_chunks > 1
    if carried:
        tail_scr, h_scr = scratch
        c = pl.program_id(1)

        @pl.when(c == 0)
        def _():
            tail_scr[...] = p0_ref[...]
            h_scr[...] = h0_ref[...]

        prev_tail = tail_scr[...]
        h_in = h_scr[...]
    else:
        prev_tail = p0_ref[...]
        h_in = h0_ref[...]

    x3 = xb_ref[...].astype(F32).reshape(groups, SUBLANES, LRU_WIDTH)
    if gps == 1:
        xprev = prev_tail
    else:
        xprev = jnp.concatenate([prev_tail, x3[:groups - 1]], axis=0)
    sub = lax.broadcasted_iota(jnp.int32, (groups, SUBLANES, LRU_WIDTH), 1)

    xc = cb_ref[...] + cw_ref[CONV_W - 1:CONV_W, :] * x3
    for shift in range(1, CONV_W):
        shifted = jnp.where(sub >= shift, pltpu.roll(x3, shift, 1), pltpu.roll(xprev, shift, 1))
        xc = xc + cw_ref[CONV_W - 1 - shift:CONV_W - shift, :] * shifted

    lam = lam_ref[...]
    neg_softplus = -(jnp.maximum(-lam, 0.0) + jnp.log1p(jnp.exp(-jnp.abs(lam))))
    z3 = zb_ref[...].astype(F32).reshape(groups, SUBLANES, LRU_WIDTH)
    xc_b = xc.reshape(rows, LRU_WIDTH).astype(BF16)
    sub_s = lax.broadcasted_iota(jnp.int32, (groups, SUBLANES, GATE_SUPER), 1)

    for sb in range(N_SUPER):
        lo, hi = sb * GATE_SUPER, (sb + 1) * GATE_SUPER
        pre = jnp.dot(xc_b[:, lo:hi], wg_ref[sb], preferred_element_type=F32)
        pre = pre.reshape(groups, SUBLANES, 2 * GATE_SUPER)
        r = _sigmoid(pre[:, :, :GATE_SUPER] + br_ref[:, lo:hi])
        gate_i = _sigmoid(pre[:, :, GATE_SUPER:] + bi_ref[:, lo:hi])
        log_a = LRU_C * r * neg_softplus[:, lo:hi]
        a = jnp.exp(log_a)
        b = jnp.sqrt(1.0 - a * a) * (gate_i * xc[:, :, lo:hi])
        for d in (1, 2, 4):
            a_s = jnp.where(sub_s >= d, pltpu.roll(a, d, 1), 1.0)
            b_s = jnp.where(sub_s >= d, pltpu.roll(b, d, 1), 0.0)
            b = a * b_s + b
            a = a * a_s
        h0 = h_in[:, :, lo:hi]
        if gps == 1:
            h = a * h0 + b
            h_last = h[:, SUBLANES - 1:SUBLANES, :]
        else:
            hs = []
            carry = h0[0]
            for g in range(groups):
                hg = a[g] * carry + b[g]
                hs.append(hg)
                carry = hg[SUBLANES - 1:SUBLANES, :]
            h = jnp.stack(hs, axis=0)
            h_last = carry[None]
        z = z3[:, :, lo:hi]
        o_ref[:, lo:hi] = (h * (z * _sigmoid(z))).reshape(rows, GATE_SUPER).astype(BF16)
        if carried:
            h_scr[:, :, lo:hi] = h_last
        hlast_ref[:, :, lo:hi] = h_last

    if gps == 1:
        tail = x3
    else:
        tail = x3[groups - 1:groups]
    if carried:
        tail_scr[...] = tail
    tail_ref[...] = tail


def _rglru(proj, p0, h0, state_index, cw, cb, wg, layer, b_r, b_i, lam, *, n_seq, seq_len, ns, chunk):
    n_chunks = seq_len // chunk
    rows = ns * chunk
    assert n_chunks == 1 or ns == 1
    assert n_chunks > 1 or chunk == SUBLANES or ns == 1
    row_map = lambda col: (lambda i, c: (i * n_chunks + c, col))
    vec = pl.BlockSpec((1, LRU_WIDTH), lambda i, c: (0, 0))
    scratch = []
    if n_chunks > 1:
        scratch = [pltpu.VMEM((1, SUBLANES, LRU_WIDTH), F32), pltpu.VMEM((1, 1, LRU_WIDTH), F32)]
    return pl.pallas_call(
        functools.partial(_lru_kernel, ns=ns, n_chunks=n_chunks),
        grid=(n_seq // ns, n_chunks),
        in_specs=[
            pl.BlockSpec((rows, LRU_WIDTH), row_map(COL_XB)),
            pl.BlockSpec((rows, LRU_WIDTH), row_map(COL_ZB)),
            pl.BlockSpec((ns, SUBLANES, LRU_WIDTH), state_index),
            pl.BlockSpec((ns, 1, LRU_WIDTH), state_index),
            pl.BlockSpec((CONV_W, LRU_WIDTH), lambda i, c: (0, 0)),
            vec,
            pl.BlockSpec((None, N_SUPER, GATE_SUPER, 2 * GATE_SUPER), lambda i, c: (layer, 0, 0, 0)),
            vec, vec, vec,
        ],
        out_specs=[
            pl.BlockSpec((rows, LRU_WIDTH), row_map(0)),
            pl.BlockSpec((ns, SUBLANES, LRU_WIDTH), lambda i, c: (i, 0, 0)),
            pl.BlockSpec((ns, 1, LRU_WIDTH), lambda i, c: (i, 0, 0)),
        ],
        out_shape=[
            jax.ShapeDtypeStruct((n_seq * seq_len, LRU_WIDTH), BF16),
            jax.ShapeDtypeStruct((n_seq, SUBLANES, LRU_WIDTH), F32),
            jax.ShapeDtypeStruct((n_seq, 1, LRU_WIDTH), F32),
        ],
        scratch_shapes=scratch,
        compiler_params=_params(2),
        name="rglru",
    )(proj, proj, p0, h0, cw, cb, wg, b_r, b_i, lam)


def _out_kernel(x_ref, oa_ref, ob_ref, ga_ref, gb_ref, wa_ref, wb_ref, wo_ref, fg_ref, o_ref, *, final):
    ya = jnp.dot(oa_ref[...], wa_ref[...], preferred_element_type=F32)
    yb = jnp.dot(ob_ref[...], wb_ref[...], preferred_element_type=F32)
    merged = (ya * _sigmoid(ga_ref[...].astype(F32)) + yb * _sigmoid(gb_ref[...].astype(F32)))
    x = x_ref[...] + jnp.dot(merged.astype(BF16), wo_ref[...], preferred_element_type=F32)
    if final:
        ms = jnp.mean(x * x, axis=-1, keepdims=True)
        x = x * lax.rsqrt(ms + EPS) * fg_ref[...]
    o_ref[...] = x


def _outproj(x, oa, ob, proj, w_a_b, w_b_b, w_o_b, layer, final_gain, tm, final):
    t = x.shape[0]
    return pl.pallas_call(
        functools.partial(_out_kernel, final=final),
        grid=(t // tm,),
        in_specs=[
            pl.BlockSpec((tm, D_MODEL), lambda i: (i, 0)),
            pl.BlockSpec((tm, RET_V), lambda i: (i, 0)),
            pl.BlockSpec((tm, LRU_WIDTH), lambda i: (i, 0)),
            pl.BlockSpec((tm, D_MODEL), lambda i: (i, COL_GA)),
            pl.BlockSpec((tm, D_MODEL), lambda i: (i, COL_GB)),
            pl.BlockSpec((None, RET_V, D_MODEL), lambda i: (layer, 0, 0)),
            pl.BlockSpec((None, LRU_WIDTH, D_MODEL), lambda i: (layer, 0, 0)),
            pl.BlockSpec((None, D_MODEL, D_MODEL), lambda i: (layer, 0, 0)),
            pl.BlockSpec((1, D_MODEL), lambda i: (0, 0)),
        ],
        out_specs=pl.BlockSpec((tm, D_MODEL), lambda i: (i, 0)),
        out_shape=jax.ShapeDtypeStruct((t, D_MODEL), F32),
        compiler_params=_params(1),
        name="outproj",
    )(x, oa, ob, proj, proj, w_a_b, w_b_b, w_o_b, final_gain)


def _rope_tables(positions):
    half = RET_DK // 2
    inv = ROPE_BASE ** (-np.arange(half, dtype=np.float64) / half)
    ang = np.asarray(positions, np.float64)[:, None] * inv[None, :]
    return jnp.asarray(np.cos(ang), F32), jnp.asarray(np.sin(ang), F32)


def _gate_weights(w_r, w_i):
    eye = jnp.eye(4, dtype=F32)[None, None, :, None, :, None]

    def super_blocks(w):
        w = w.reshape(DEPTH, N_SUPER, 4, LRU_BLOCK, LRU_BLOCK)
        return (w[:, :, :, :, None, :] * eye).reshape(DEPTH, N_SUPER, GATE_SUPER, GATE_SUPER)

    return jnp.concatenate([super_blocks(w_r), super_blocks(w_i)], axis=-1).astype(BF16)


def kernel(x_prompt, x_sample, state_ret, state_conv, state_rglru, meta_tokens, ln_gain, w_in, conv_w, conv_b,
           w_r, b_r, w_i, b_i, lru_lambda, gn_gain, w_a, w_b, w_o, final_gain):
    batch, seq, _ = x_prompt.shape
    dec_batch, dec_seq, _ = x_sample.shape
    assert seq % CHUNK == 0 and dec_seq == SUBLANES and N_META % SUBLANES == 0

    w_in_b = w_in.astype(BF16)
    w_a_b = w_a.astype(BF16)
    w_b_b = w_b.astype(BF16)
    w_o_b = w_o.astype(BF16)
    wg = _gate_weights(w_r, w_i)
    fg = final_gain.reshape(1, D_MODEL)

    cos_m, sin_m = _rope_tables(np.arange(N_META))
    cos_p, sin_p = _rope_tables(N_META + np.arange(seq))
    cos_s, sin_s = _rope_tables(PAST_LEN + np.tile(np.arange(dec_seq), dec_batch))

    xm = meta_tokens
    xp = x_prompt.reshape(batch * seq, D_MODEL)
    xs = x_sample.reshape(dec_batch * dec_seq, D_MODEL)
    tm_p = 1024
    tm_s = dec_batch * dec_seq
    lru_ns_s = 16
    ret_ns_s = 2
    lru_chunk_p = 256

    zero_state = jnp.zeros((1, RET_HEADS, RET_DK, RET_DV), F32)
    zero_tail = jnp.zeros((1, SUBLANES, LRU_WIDTH), F32)
    zero_h = jnp.zeros((1, 1, LRU_WIDTH), F32)
    conv_hist = jnp.pad(state_conv, ((0, 0), (0, 0), (SUBLANES - (CONV_W - 1), 0), (0, 0)))
    h_hist = state_rglru.reshape(DEPTH, dec_batch, 1, LRU_WIDTH)

    state_shape_s = (DEPTH, dec_batch, RET_HEADS, RET_DK, RET_DV)
    ret_s_all = None
    ret_p, conv_p, h_p, conv_s, h_s = [], [], [], [], []
    for l in range(DEPTH):
        ln_g = ln_gain[l].reshape(1, D_MODEL)
        gn_g = gn_gain[l].reshape(1, RET_V)
        lru_args = (conv_w[l], conv_b[l].reshape(1, LRU_WIDTH), wg, l, b_r[l].reshape(1, LRU_WIDTH),
                    b_i[l].reshape(1, LRU_WIDTH), lru_lambda[l].reshape(1, LRU_WIDTH))
        first3 = lambda i, c: (0, 0, 0)
        per_step3 = lambda i, c: (i, 0, 0)

        pm = _inproj(xm, ln_g, w_in_b, l, cos_m, sin_m, N_META)
        oa_m, s_m = _retention(
            pm, gn_g, zero_state,
            pl.BlockSpec((1, RET_HEADS, RET_DK, RET_DV), lambda i, c: (0, 0, 0, 0)),
            n_seq=1, seq_len=N_META, ns=1, chunk=N_META,
            snew_shape=(1, RET_HEADS, RET_DK, RET_DV),
            snew_index=pl.BlockSpec((1, RET_HEADS, RET_DK, RET_DV), lambda i, c: (i, 0, 0, 0)))
        ob_m, tail_m, h_m = _rglru(pm, zero_tail, zero_h, first3, *lru_args,
                                   n_seq=1, seq_len=N_META, ns=1, chunk=N_META)
        if l + 1 < DEPTH:
            xm = _outproj(xm, oa_m, ob_m, pm, w_a_b, w_b_b, w_o_b, l, fg, N_META, False)

        pp = _inproj(xp, ln_g, w_in_b, l, cos_p, sin_p, tm_p)
        oa_p, s_p = _retention(
            pp, gn_g, s_m,
            pl.BlockSpec((1, RET_HEADS, RET_DK, RET_DV), lambda i, c: (0, 0, 0, 0)),
            n_seq=batch, seq_len=seq, ns=1, chunk=CHUNK,
            snew_shape=(batch, RET_HEADS, RET_DK, RET_DV),
            snew_index=pl.BlockSpec((1, RET_HEADS, RET_DK, RET_DV), lambda i, c: (i, 0, 0, 0)))
        ob_p, tail_p, hl_p = _rglru(pp, tail_m, h_m, first3, *lru_args,
                                    n_seq=batch, seq_len=seq, ns=1, chunk=lru_chunk_p)
        xp = _outproj(xp, oa_p, ob_p, pp, w_a_b, w_b_b, w_o_b, l, fg, tm_p, l + 1 == DEPTH)
        ret_p.append(s_p)
        conv_p.append(tail_p[:, SUBLANES - (CONV_W - 1):, :])
        h_p.append(hl_p.reshape(batch, LRU_WIDTH))

        ps = _inproj(xs, ln_g, w_in_b, l, cos_s, sin_s, tm_s)
        oa_s, ret_s_all = _retention(
            ps, gn_g, state_ret,
            pl.BlockSpec((None, ret_ns_s, RET_HEADS, RET_DK, RET_DV), lambda i, c, l=l: (l, i, 0, 0, 0)),
            n_seq=dec_batch, seq_len=dec_seq, ns=ret_ns_s, chunk=dec_seq,
            snew_shape=state_shape_s,
            snew_index=pl.BlockSpec((None, ret_ns_s, RET_HEADS, RET_DK, RET_DV),
                                    lambda i, c, l=l: (l, i, 0, 0, 0)),
            snew_prev=ret_s_all)
        ob_s, tail_s, hl_s = _rglru(ps, conv_hist[l], h_hist[l], per_step3, *lru_args,
                                    n_seq=dec_batch, seq_len=dec_seq, ns=lru_ns_s, chunk=dec_seq)
        xs = _outproj(xs, oa_s, ob_s, ps, w_a_b, w_b_b, w_o_b, l, fg, 256, l + 1 == DEPTH)
        conv_s.append(tail_s[:, SUBLANES - (CONV_W - 1):, :])
        h_s.append(hl_s.reshape(dec_batch, LRU_WIDTH))

    return (xp.reshape(batch, seq, D_MODEL), xs.reshape(dec_batch, dec_seq, D_MODEL),
            jnp.stack(ret_p), jnp.stack(conv_p), jnp.stack(h_p),
            ret_s_all, jnp.stack(conv_s), jnp.stack(h_s))
```
